```python
import math
import jax, jax.numpy as jnp
from jax import lax
import numpy as np

D_MODEL = 1024
BATCH = 8
SEQ = 8192
DEPTH = 4

MEM_LEN = 256
D_MIX = D_MODEL
D_RWKV = D_MIX // 2
D_CONV = D_MIX - D_RWKV
RWKV_HEAD = 64
RWKV_HEADS = D_RWKV // RWKV_HEAD
CONV_WIDTH = 31
DECAY_LORA = 64
AAA_LORA = 64
MV_LORA = 32
GATE_LORA = 128
XA_HEADS = 4
XA_HEAD_DIM = D_MODEL // XA_HEADS
D_FF = -(-8 * D_MODEL // (3 * 256)) * 256
RMS_EPS = 1e-6
LN_EPS = 1e-5
GN_EPS = 64e-5
DECAY_SCALE = math.exp(-0.5)

N_RWKV_COLS = 3 * D_RWKV + DECAY_LORA + AAA_LORA + GATE_LORA
N_IN_FIRST = 2 * D_CONV + N_RWKV_COLS
N_IN_REST = N_IN_FIRST + MV_LORA

kernel_name = "hymba_rwkv7_conformer_memxattn_trunk"


def rms_norm(x, g):
    xf = x.astype(jnp.float32)
    y = xf * lax.rsqrt(jnp.mean(xf * xf, axis=-1, keepdims=True) + RMS_EPS)
    return (y * g.astype(jnp.float32)).astype(x.dtype)


def layer_norm(x, g, b):
    xf = x.astype(jnp.float32)
    mean = jnp.mean(xf, axis=-1, keepdims=True)
    var = jnp.mean(jnp.square(xf - mean), axis=-1, keepdims=True)
    y = (xf - mean) * lax.rsqrt(var + LN_EPS)
    return (y * g.astype(jnp.float32) + b.astype(jnp.float32)).astype(x.dtype)


def token_shift_mix(p, mu):
    prev = jnp.pad(p, ((0, 0), (1, 0), (0, 0)))[:, :-1]
    return p + mu * (prev - p)


def conformer_conv(u, conv_w, conv_b, ln_g, ln_b):
    h = u[..., :D_CONV] * jax.nn.sigmoid(u[..., D_CONV:])
    h = lax.conv_general_dilated(
        h, conv_w[:, None, :], window_strides=(1,), padding=[(CONV_WIDTH - 1, 0)],
        dimension_numbers=("NWC", "WIO", "NWC"), feature_group_count=D_CONV) + conv_b
    h = layer_norm(h, ln_g, ln_b)
    return jax.nn.silu(h)


def rwkv7_recurrence(r, w, k, v, kk, a):
    B = r.shape[0]
    tm = lambda z: jnp.moveaxis(z.astype(jnp.float32), 1, 0)

    def step(S, inp):
        r_t, w_t, k_t, v_t, kk_t, a_t = inp
        s_kk = jnp.einsum("bhvk,bhk->bhv", S, kk_t)
        S = (S * w_t[:, :, None, :]
             - s_kk[..., None] * (kk_t * a_t)[:, :, None, :]
             + v_t[..., None] * k_t[:, :, None, :])
        return S, jnp.einsum("bhvk,bhk->bhv", S, r_t)

    S0 = jnp.zeros((B, RWKV_HEADS, RWKV_HEAD, RWKV_HEAD), jnp.float32)
    _, y = lax.scan(step, S0, (tm(r), tm(w), tm(k), tm(v), tm(kk), tm(a)))
    return jnp.moveaxis(y, 0, 1)


def rwkv7_mix(p, mu, w0, w_up, a0, a_up, g_up, k_k, k_a, r_k, lnx_g, lnx_b, v_first, v0, v_up):
    B, T = p.shape[:2]
    p = token_shift_mix(p, mu)
    o = 0
    r = p[..., o:o + D_RWKV]; o += D_RWKV
    k = p[..., o:o + D_RWKV]; o += D_RWKV
    v = p[..., o:o + D_RWKV]; o += D_RWKV
    w_lo = p[..., o:o + DECAY_LORA]; o += DECAY_LORA
    a_lo = p[..., o:o + AAA_LORA]; o += AAA_LORA
    g_lo = p[..., o:o + GATE_LORA]; o += GATE_LORA

    log_w = -DECAY_SCALE * jax.nn.sigmoid((w0 + jnp.tanh(w_lo) @ w_up).astype(jnp.float32))
    w = jnp.exp(log_w)
    a = jax.nn.sigmoid(a0 + a_lo @ a_up)
    g = jax.nn.sigmoid(g_lo) @ g_up
    if v_first is None:
        v_first = v
    else:
        vres_lo = p[..., o:o + MV_LORA]
        v = v + (v_first - v) * jax.nn.sigmoid(v0 + vres_lo @ v_up)

    hd = lambda z: z.reshape(B, T, RWKV_HEADS, RWKV_HEAD)
    kk = hd(k * k_k).astype(jnp.float32)
    kk = kk * lax.rsqrt(jnp.maximum(jnp.sum(kk * kk, axis=-1, keepdims=True), 1e-24))
    k = k * (1.0 + (a - 1.0) * k_a)
    rh, kh, vh = hd(r), hd(k), hd(v)

    y = rwkv7_recurrence(rh, hd(w), kh, vh, kk, hd(a))
    mean = jnp.mean(y, axis=-1, keepdims=True)
    var = jnp.mean(jnp.square(y - mean), axis=-1, keepdims=True)
    y = (y - mean) * lax.rsqrt(var + GN_EPS)
    gn_g = lnx_g.reshape(RWKV_HEADS, RWKV_HEAD).astype(jnp.float32)
    gn_b = lnx_b.reshape(RWKV_HEADS, RWKV_HEAD).astype(jnp.float32)
    y = y * gn_g + gn_b
    bonus = jnp.sum((rh * kh * r_k).astype(jnp.float32), axis=-1, keepdims=True) * vh.astype(jnp.float32)
    y = (y + bonus).astype(p.dtype).reshape(B, T, D_RWKV) * g
    return y, v_first


def memory_cross_attention(h, mem_n, wq, wkv, wo):
    B, T, _ = h.shape
    M = mem_n.shape[1]
    q = (h @ wq).reshape(B, T, XA_HEADS, XA_HEAD_DIM)
    kv = mem_n @ wkv
    km = kv[..., :D_MODEL].reshape(B, M, XA_HEADS, XA_HEAD_DIM)
    vm = kv[..., D_MODEL:].reshape(B, M, XA_HEADS, XA_HEAD_DIM)
    s = jnp.einsum("bthd,bmhd->bhtm", q, km).astype(jnp.float32) * (XA_HEAD_DIM ** -0.5)
    pr = jax.nn.softmax(s, axis=-1).astype(vm.dtype)
    o = jnp.einsum("bhtm,bmhd->bthd", pr, vm).reshape(B, T, D_MODEL)
    return o @ wo


def swiglu(h, w_gu, w_down):
    gu = h @ w_gu
    return (jax.nn.silu(gu[..., :D_FF]) * gu[..., D_FF:]) @ w_down


def setup_inputs(seed: int = 0) -> dict:
    key = jax.random.key(seed)
    ks = iter(jax.random.split(key, 40))
    nrm = lambda shape, s: jax.random.normal(next(ks), shape, jnp.float32) * s
    uni = lambda shape: jax.random.uniform(next(ks), shape, jnp.float32)
    L, Lr = DEPTH, DEPTH - 1
    return {
        "x": nrm((BATCH, SEQ, D_MODEL), 1.0),
        "mem": nrm((BATCH, MEM_LEN, D_MODEL), 1.0),
        "mem_norm_g": 1.0 + nrm((D_MODEL,), 0.05),
        "norm_gains": 1.0 + nrm((L, 6, D_MODEL), 0.05),
        "w_in_first": nrm((D_MODEL, N_IN_FIRST), D_MODEL ** -0.5),
        "w_in_rest": nrm((Lr, D_MODEL, N_IN_REST), D_MODEL ** -0.5),
        "mu_first": uni((N_RWKV_COLS,)),
        "mu_rest": uni((Lr, N_RWKV_COLS + MV_LORA)),
        "conv_w": nrm((L, CONV_WIDTH, D_CONV), CONV_WIDTH ** -0.5),
        "conv_b": nrm((L, D_CONV), 0.02),
        "conv_ln_g": 1.0 + nrm((L, D_CONV), 0.05),
        "conv_ln_b": nrm((L, D_CONV), 0.02),
        "w0": nrm((L, D_RWKV), 1.0),
        "w_up": nrm((L, DECAY_LORA, D_RWKV), DECAY_LORA ** -0.5),
        "a0": nrm((L, D_RWKV), 0.1),
        "a_up": nrm((L, AAA_LORA, D_RWKV), AAA_LORA ** -0.5),
        "g_up": nrm((L, GATE_LORA, D_RWKV), GATE_LORA ** -0.5),
        "v0": nrm((Lr, D_RWKV), 0.1),
        "v_up": nrm((Lr, MV_LORA, D_RWKV), MV_LORA ** -0.5),
        "k_k": 0.85 + nrm((L, D_RWKV), 0.05),
        "k_a": 1.0 + nrm((L, D_RWKV), 0.05),
        "r_k": nrm((L, RWKV_HEADS, RWKV_HEAD), 0.1),
        "lnx_g": 1.0 + nrm((L, D_RWKV), 0.05),
        "lnx_b": nrm((L, D_RWKV), 0.02),
        "w_out": nrm((L, D_MIX, D_MODEL), D_MIX ** -0.5),
        "wq": nrm((L, D_MODEL, D_MODEL), D_MODEL ** -0.5),
        "wkv": nrm((L, D_MODEL, 2 * D_MODEL), D_MODEL ** -0.5),
        "wo": nrm((L, D_MODEL, D_MODEL), D_MODEL ** -0.5),
        "w_gu": nrm((L, D_MODEL, 2 * D_FF), D_MODEL ** -0.5),
        "w_down": nrm((L, D_FF, D_MODEL), D_FF ** -0.5),
    }


def reference(x, mem, mem_norm_g, norm_gains, w_in_first, w_in_rest, mu_first, mu_rest,
              conv_w, conv_b, conv_ln_g, conv_ln_b, w0, w_up, a0, a_up, g_up, v0, v_up,
              k_k, k_a, r_k, lnx_g, lnx_b, w_out, wq, wkv, wo, w_gu, w_down):
    mem_n = rms_norm(mem, mem_norm_g)
    v_first = None
    for i in range(DEPTH):
        g = norm_gains[i]
        h = rms_norm(x, g[0])
        if i == 0:
            p = h @ w_in_first
            mu, vr0, vr_up = mu_first, None, None
        else:
            p = h @ w_in_rest[i - 1]
            mu, vr0, vr_up = mu_rest[i - 1], v0[i - 1], v_up[i - 1]
        y_conv = conformer_conv(p[..., :2 * D_CONV], conv_w[i], conv_b[i], conv_ln_g[i], conv_ln_b[i])
        y_rwkv, v_first = rwkv7_mix(p[..., 2 * D_CONV:], mu, w0[i], w_up[i], a0[i], a_up[i], g_up[i],
                                    k_k[i], k_a[i], r_k[i], lnx_g[i], lnx_b[i], v_first, vr0, vr_up)
        y = jnp.concatenate([y_conv, y_rwkv], axis=-1) @ w_out[i]
        x = x + rms_norm(y, g[1])
        y = memory_cross_attention(rms_norm(x, g[2]), mem_n, wq[i], wkv[i], wo[i])
        x = x + rms_norm(y, g[3])
        y = swiglu(rms_norm(x, g[4]), w_gu[i], w_down[i])
        x = x + rms_norm(y, g[5])
    return x
```

```python
import functools
import math

import jax
import jax.numpy as jnp
from jax import lax
from jax.experimental import pallas as pl
from jax.experimental.pallas import tpu as pltpu

F32 = jnp.float32
BF16 = jnp.bfloat16

RMS_EPS = 1e-6
LN_EPS = 1e-5
GN_EPS = 64e-5
DECAY_SCALE = math.exp(-0.5)

D_MODEL = 1024
D_CONV = 512
D_RWKV = 512
HEAD = 64
CONV_WIDTH = 31
XA_HEADS = 4
XA_HEAD_DIM = D_MODEL // XA_HEADS
LORA_PAD = 128
RWKV_COLS_PAD = 3 * D_RWKV + 3 * LORA_PAD

CHUNK = 64
GROUP = 4 * HEAD
CONV_HIST = 32

VMEM_LIMIT_BYTES = 56 * 1024 * 1024


def _params(sem):
    return pltpu.CompilerParams(dimension_semantics=sem, vmem_limit_bytes=VMEM_LIMIT_BYTES)


def _dot(a, b):
    return jnp.dot(a, b, preferred_element_type=F32)


def _dot_nt(a, b):
    return lax.dot_general(a, b, (((1,), (1,)), ((), ())), preferred_element_type=F32)


def _dot_tn(a, b):
    return lax.dot_general(a, b, (((0,), (0,)), ((), ())), preferred_element_type=F32)


def _rms(x, g):
    return x * lax.rsqrt(jnp.mean(x * x, axis=-1, keepdims=True) + RMS_EPS) * g


def _split_dot_rhs(lhs_bf16, x):
    hi = x.astype(BF16)
    lo = (x - hi.astype(F32)).astype(BF16)
    return _dot(lhs_bf16, hi) + _dot(lhs_bf16, lo)


def _split_dot_lhs(x, rhs_bf16):
    hi = x.astype(BF16)
    lo = (x - hi.astype(F32)).astype(BF16)
    return _dot(hi, rhs_bf16) + _dot(lo, rhs_bf16)


def _const_spec(shape):
    nd = len(shape)
    return pl.BlockSpec(shape, lambda *_: (0,) * nd)


def _in_proj_kernel(x_ref, g_ref, wc_ref, wr_ref, pc_ref, pr_ref):
    h = _rms(x_ref[...], g_ref[...]).astype(BF16)
    pc_ref[...] = _dot(h, wc_ref[...])
    pr_ref[...] = _dot(h, wr_ref[...])


def _in_proj(x2, g, wc, wr, tm):
    n = x2.shape[0]
    return pl.pallas_call(
        _in_proj_kernel,
        grid=(n // tm,),
        in_specs=[
            pl.BlockSpec((tm, D_MODEL), lambda i: (i, 0)),
            _const_spec((1, D_MODEL)),
            _const_spec(wc.shape),
            _const_spec(wr.shape),
        ],
        out_specs=[
            pl.BlockSpec((tm, 2 * D_CONV), lambda i: (i, 0)),
            pl.BlockSpec((tm, RWKV_COLS_PAD), lambda i: (i, 0)),
        ],
        out_shape=[
            jax.ShapeDtypeStruct((n, 2 * D_CONV), F32),
            jax.ShapeDtypeStruct((n, RWKV_COLS_PAD), F32),
        ],
        compiler_params=_params(("arbitrary",)),
        name="in_proj",
    )(x2, g, wc, wr)


def _conv_kernel(tt, pc_ref, cw_ref, cb_ref, lg_ref, lb_ref, o_ref, hbuf):
    t = pl.program_id(1)

    @pl.when(t == 0)
    def _():
        hbuf[0:CONV_HIST, :] = jnp.zeros((CONV_HIST, D_CONV), F32)

    @pl.when(t > 0)
    def _():
        hbuf[0:CONV_HIST, :] = hbuf[tt:tt + CONV_HIST, :]

    u = pc_ref[...]
    hbuf[CONV_HIST:, :] = u[:, :D_CONV] * jax.nn.sigmoid(u[:, D_CONV:])

    rb = 32
    base = CONV_HIST - (CONV_WIDTH - 1)
    for r0 in range(0, tt, rb):
        acc = jnp.broadcast_to(cb_ref[...], (rb, D_CONV))
        for j in range(CONV_WIDTH):
            acc = acc + hbuf[r0 + base + j:r0 + base + j + rb, :] * cw_ref[j:j + 1, :]
        mean = jnp.mean(acc, axis=-1, keepdims=True)
        d = acc - mean
        var = jnp.mean(d * d, axis=-1, keepdims=True)
        y = d * lax.rsqrt(var + LN_EPS) * lg_ref[...] + lb_ref[...]
        o_ref[r0:r0 + rb, :] = (y * jax.nn.sigmoid(y)).astype(o_ref.dtype)


def _conv(pc3, cw, cb, lg, lb, tt):
    b, t, _ = pc3.shape
    return pl.pallas_call(
        functools.partial(_conv_kernel, tt),
        grid=(b, t // tt),
        in_specs=[
            pl.BlockSpec((None, tt, 2 * D_CONV), lambda i, j: (i, j, 0)),
            _const_spec(cw.shape),
            _const_spec((1, D_CONV)),
            _const_spec((1, D_CONV)),
            _const_spec((1, D_CONV)),
        ],
        out_specs=pl.BlockSpec((None, tt, D_CONV), lambda i, j: (i, j, 0)),
        out_shape=jax.ShapeDtypeStruct((b, t, D_CONV), BF16),
        scratch_shapes=[pltpu.VMEM((tt + CONV_HIST, D_CONV), F32)],
        compiler_params=_params(("arbitrary", "arbitrary")),
        name="conformer_conv",
    )(pc3, cw, cb, lg, lb)


def _block_diag(z, same_head):
    tiled = jnp.concatenate([z] * (GROUP // CHUNK), axis=0)
    return jnp.where(same_head, tiled, 0.0).astype(BF16)


def _rwkv_chunk(lw, r, k, v, aa, bb, s_prev, ltri, same_head, strict, incl, eye):
    c = CHUNK
    cin = _split_dot_rhs(ltri, lw)
    cex = cin - lw
    clast = cin[c - 1:c, :]
    a_t = aa * jnp.exp(cex)
    r_t = r * jnp.exp(cin)
    ginv = jnp.exp(-cin)
    b_t = bb * ginv
    k_t = k * ginv
    gl = jnp.exp(clast - cin)
    b_p = bb * gl
    k_p = k * gl

    x = jnp.concatenate([a_t, r_t], axis=0).astype(BF16)
    bk = jnp.concatenate([_block_diag(b_t, same_head), _block_diag(k_t, same_head)], axis=0)
    a_all = _dot_nt(x, bk)
    a_ab = jnp.where(strict, a_all[:c, :GROUP], 0.0)
    a_ak = jnp.where(strict, a_all[:c, GROUP:], 0.0)
    a_rb = jnp.where(incl, a_all[c:, :GROUP], 0.0)
    a_rk = jnp.where(incl, a_all[c:, GROUP:], 0.0)

    q = jnp.where(eye, 1.0, 0.0) + a_ab
    ak = _dot(a_ab.astype(BF16), _block_diag(a_ab, same_head))
    n_levels = int(math.log2(c))
    for _ in range(n_levels - 2):
        qa = _dot(jnp.concatenate([q, ak], axis=0).astype(BF16), _block_diag(ak, same_head))
        q = q + qa[:c]
        ak = qa[c:]
    inv = q + _dot(q.astype(BF16), _block_diag(ak, same_head))

    v_blk = _block_diag(v, same_head)
    z = _dot(a_ak.astype(BF16), v_blk)
    wu = _dot(inv.astype(BF16),
              jnp.concatenate([_block_diag(a_t, same_head), _block_diag(z, same_head)], axis=1))
    w = wu[:, :GROUP]
    u0 = wu[:, GROUP:]

    wr = _dot_nt(jnp.concatenate([w, r_t], axis=0).astype(BF16), s_prev.astype(BF16))
    u = wr[:c] + u0
    y = wr[c:] + _dot(jnp.concatenate([a_rb, a_rk], axis=1).astype(BF16),
                      jnp.concatenate([_block_diag(u, same_head), v_blk], axis=0))
    uv = jnp.concatenate([u, v], axis=0).astype(BF16)
    bkp = jnp.concatenate([b_p, k_p], axis=0).astype(BF16)
    s_new = s_prev * jnp.exp(clast) + jnp.where(same_head, _dot_tn(uv, bkp), 0.0)
    return y, s_new


def _rwkv_kernel(tt, has_vfirst, *refs):
    if has_vfirst:
        (pr_ref, vf_ref, mu_ref, w0_ref, wup_ref, a0_ref, aup_ref, gup_ref, v0_ref, vup_ref,
         kk_ref, ka_ref, rk_ref, lng_ref, lnb_ref, hs_ref, ltri_ref,
         y_ref,
         carry, state, r_s, k_s, v_s, lw_s, aa_s, bb_s, bonus_s, g_s, y_s) = refs
    else:
        (pr_ref, mu_ref, w0_ref, wup_ref, a0_ref, aup_ref, gup_ref,
         kk_ref, ka_ref, rk_ref, lng_ref, lnb_ref, hs_ref, ltri_ref,
         y_ref, vfo_ref,
         carry, state, r_s, k_s, v_s, lw_s, aa_s, bb_s, bonus_s, g_s, y_s) = refs

    @pl.when(pl.program_id(1) == 0)
    def _():
        carry[...] = jnp.zeros_like(carry)
        state[...] = jnp.zeros_like(state)

    hs = hs_ref[...]

    def head_sum(x):
        return _split_dot_lhs(x, hs)

    p = pr_ref[...]
    rolled = pltpu.roll(p, 1, 0)
    row = lax.broadcasted_iota(jnp.int32, p.shape, 0)
    prev = jnp.where(row == 0, carry[0:1, :], rolled)
    carry[0:1, :] = p[tt - 1:tt, :]
    ps = p + mu_ref[...] * (prev - p)

    r = ps[:, 0:D_RWKV]
    k = ps[:, D_RWKV:2 * D_RWKV]
    v = ps[:, 2 * D_RWKV:3 * D_RWKV]
    o = 3 * D_RWKV
    wa_lo = ps[:, o:o + LORA_PAD]
    g_lo = ps[:, o + LORA_PAD:o + 2 * LORA_PAD]
    lw = -DECAY_SCALE * jax.nn.sigmoid(w0_ref[...] + _dot(jnp.tanh(wa_lo).astype(BF16), wup_ref[...]))
    eta = jax.nn.sigmoid(a0_ref[...] + _dot(wa_lo.astype(BF16), aup_ref[...]))
    g_s[...] = _dot(jax.nn.sigmoid(g_lo).astype(BF16), gup_ref[...])
    if has_vfirst:
        v_lo = ps[:, o + 2 * LORA_PAD:o + 3 * LORA_PAD]
        v = v + (vf_ref[...] - v) * jax.nn.sigmoid(v0_ref[...] + _dot(v_lo.astype(BF16), vup_ref[...]))
    else:
        vfo_ref[...] = v
    kk = k * kk_ref[...]
    kk = kk * lax.rsqrt(jnp.maximum(head_sum(kk * kk), 1e-24))
    k = k * (1.0 + (eta - 1.0) * ka_ref[...])
    bonus_s[...] = head_sum(r * k * rk_ref[...]) * v
    r_s[...] = r
    k_s[...] = k
    v_s[...] = v
    lw_s[...] = lw
    aa_s[...] = kk
    bb_s[...] = -(kk * eta)

    gi = lax.broadcasted_iota(jnp.int32, (GROUP, GROUP), 0) // HEAD
    gj = lax.broadcasted_iota(jnp.int32, (GROUP, GROUP), 1) // HEAD
    same_head = gi == gj
    ti = lax.broadcasted_iota(jnp.int32, (CHUNK, GROUP), 0)
    si = lax.broadcasted_iota(jnp.int32, (CHUNK, GROUP), 1) % CHUNK
    strict = si < ti
    incl = si <= ti
    eye = si == ti
    ltri = ltri_ref[...]
    for grp in range(D_RWKV // GROUP):
        lanes = slice(grp * GROUP, (grp + 1) * GROUP)
        s_cur = state[grp]
        for c0 in range(0, tt, CHUNK):
            rows = slice(c0, c0 + CHUNK)
            y, s_cur = _rwkv_chunk(lw_s[rows, lanes], r_s[rows, lanes], k_s[rows, lanes],
                                   v_s[rows, lanes], aa_s[rows, lanes], bb_s[rows, lanes],
                                   s_cur, ltri, same_head, strict, incl, eye)
            y_s[rows, lanes] = y
        state[grp] = s_cur

    y = y_s[...]
    mean = head_sum(y) * (1.0 / HEAD)
    d = y - mean
    var = head_sum(d * d) * (1.0 / HEAD)
    yn = d * lax.rsqrt(var + GN_EPS) * lng_ref[...] + lnb_ref[...]
    y_ref[...] = ((yn + bonus_s[...]) * g_s[...]).astype(y_ref.dtype)


def _rwkv(pr3, vfirst3, prm, tt):
    b, t, _ = pr3.shape
    has_vfirst = vfirst3 is not None
    tile = lambda w: pl.BlockSpec((None, tt, w), lambda i, j: (i, j, 0))
    row = _const_spec((1, D_RWKV))
    lora = _const_spec((LORA_PAD, D_RWKV))
    in_specs = [tile(RWKV_COLS_PAD)]
    args = [pr3]
    if has_vfirst:
        in_specs.append(tile(D_RWKV))
        args.append(vfirst3)
    in_specs += [_const_spec((1, RWKV_COLS_PAD)), row, lora, row, lora, lora]
    args += [prm["mu"], prm["w0"], prm["w_up"], prm["a0"], prm["a_up"], prm["g_up"]]
    if has_vfirst:
        in_specs += [row, lora]
        args += [prm["v0"], prm["v_up"]]
    in_specs += [row, row, row, row, row, _const_spec((D_RWKV, D_RWKV)), _const_spec((CHUNK, CHUNK))]
    args += [prm["k_k"], prm["k_a"], prm["r_k"], prm["lnx_g"], prm["lnx_b"], prm["head_sum"], prm["ltri"]]
    out_specs = [tile(D_RWKV)]
    out_shape = [jax.ShapeDtypeStruct((b, t, D_RWKV), BF16)]
    if not has_vfirst:
        out_specs.append(tile(D_RWKV))
        out_shape.append(jax.ShapeDtypeStruct((b, t, D_RWKV), F32))
    work = pltpu.VMEM((tt, D_RWKV), F32)
    outs = pl.pallas_call(
        functools.partial(_rwkv_kernel, tt, has_vfirst),
        grid=(b, t // tt),
        in_specs=in_specs,
        out_specs=out_specs,
        out_shape=out_shape,
        scratch_shapes=[pltpu.VMEM((8, RWKV_COLS_PAD), F32),
                        pltpu.VMEM((D_RWKV // GROUP, GROUP, GROUP), F32)] + [work] * 9,
        compiler_params=_params(("arbitrary", "arbitrary")),
        name="rwkv7_mix",
    )(*args)
    if has_vfirst:
        return outs[0], vfirst3
    return outs[0], outs[1]


def _out_proj_kernel(yc_ref, yr_ref, x_ref, wt_ref, wb_ref, g_ref, o_ref):
    y = _dot(yc_ref[...], wt_ref[...]) + _dot(yr_ref[...], wb_ref[...])
    o_ref[...] = x_ref[...] + _rms(y, g_ref[...])


def _out_proj(yc2, yr2, x2, wt, wb, g, tm):
    n = x2.shape[0]
    half = pl.BlockSpec((tm, D_CONV), lambda i: (i, 0))
    full = pl.BlockSpec((tm, D_MODEL), lambda i: (i, 0))
    return pl.pallas_call(
        _out_proj_kernel,
        grid=(n // tm,),
        in_specs=[half, half, full, _const_spec(wt.shape), _const_spec(wb.shape), _const_spec((1, D_MODEL))],
        out_specs=full,
        out_shape=jax.ShapeDtypeStruct((n, D_MODEL), F32),
        input_output_aliases={2: 0},
        compiler_params=_params(("arbitrary",)),
        name="mixer_out_proj",
    )(yc2, yr2, x2, wt, wb, g)


def _mem_kv_kernel(mem_ref, g_ref, wkv_ref, kv_ref):
    mn = _rms(mem_ref[...], g_ref[...]).astype(BF16)
    kv_ref[...] = _dot(mn, wkv_ref[...]).astype(kv_ref.dtype)


def _mem_kv(mem2, g, wkv):
    layers = wkv.shape[0]
    m = mem2.shape[0]
    return pl.pallas_call(
        _mem_kv_kernel,
        grid=(layers,),
        in_specs=[_const_spec(mem2.shape), _const_spec((1, D_MODEL)),
                  pl.BlockSpec((None, D_MODEL, 2 * D_MODEL), lambda l: (l, 0, 0))],
        out_specs=pl.BlockSpec((None, m, 2 * D_MODEL), lambda l: (l, 0, 0)),
        out_shape=jax.ShapeDtypeStruct((layers, m, 2 * D_MODEL), BF16),
        compiler_params=_params(("arbitrary",)),
        name="memory_kv",
    )(mem2, g, wkv)


def _xattn_kernel(x_ref, kv_ref, gpre_ref, wq_ref, wo_ref, gpost_ref, o_ref):
    x = x_ref[...]
    h = _rms(x, gpre_ref[...]).astype(BF16)
    q = (_dot(h, wq_ref[...]) * (XA_HEAD_DIM ** -0.5)).astype(BF16)
    outs = []
    for hd in range(XA_HEADS):
        lo = hd * XA_HEAD_DIM
        s = _dot_nt(q[:, lo:lo + XA_HEAD_DIM], kv_ref[:, lo:lo + XA_HEAD_DIM])
        e = jnp.exp(s - jnp.max(s, axis=-1, keepdims=True))
        pr = (e / jnp.sum(e, axis=-1, keepdims=True)).astype(BF16)
        outs.append(_dot(pr, kv_ref[:, D_MODEL + lo:D_MODEL + lo + XA_HEAD_DIM]))
    o = jnp.concatenate(outs, axis=-1).astype(BF16)
    o_ref[...] = x + _rms(_dot(o, wo_ref[...]), gpost_ref[...])


def _xattn(x3, kv3, gpre, wq, wo, gpost, tq):
    b, t, _ = x3.shape
    m = kv3.shape[1]
    xs = pl.BlockSpec((None, tq, D_MODEL), lambda i, j: (i, j, 0))
    return pl.pallas_call(
        _xattn_kernel,
        grid=(b, t // tq),
        in_specs=[xs, pl.BlockSpec((None, m, 2 * D_MODEL), lambda i, j: (i, 0, 0)),
                  _const_spec((1, D_MODEL)), _const_spec(wq.shape), _const_spec(wo.shape),
                  _const_spec((1, D_MODEL))],
        out_specs=xs,
        out_shape=jax.ShapeDtypeStruct(x3.shape, F32),
        input_output_aliases={0: 0},
        compiler_params=_params(("arbitrary", "arbitrary")),
        name="memory_xattn",
    )(x3, kv3, gpre, wq, wo, gpost)


def _swiglu_kernel(d_ff, ff_chunk, x_ref, gpre_ref, wgu_ref, wd_ref, gpost_ref, o_ref):
    x = x_ref[...]
    h = _rms(x, gpre_ref[...]).astype(BF16)
    acc = jnp.zeros(x.shape, F32)
    for c0 in range(0, d_ff, ff_chunk):
        gate = _dot(h, wgu_ref[:, c0:c0 + ff_chunk])
        up = _dot(h, wgu_ref[:, d_ff + c0:d_ff + c0 + ff_chunk])
        act = (gate * jax.nn.sigmoid(gate) * up).astype(BF16)
        acc = acc + _dot(act, wd_ref[c0:c0 + ff_chunk, :])
    o_ref[...] = x + _rms(acc, gpost_ref[...])


def _swiglu(x2, gpre, wgu, wd, gpost, tm):
    n = x2.shape[0]
    d_ff = wd.shape[0]
    ff_chunk = next(c for c in (1408, 1024, 512, 256, 128) if d_ff % c == 0)
    xs = pl.BlockSpec((tm, D_MODEL), lambda i: (i, 0))
    single = dict(pipeline_mode=pl.Buffered(1))
    return pl.pallas_call(
        functools.partial(_swiglu_kernel, d_ff, ff_chunk),
        grid=(n // tm,),
        in_specs=[xs, _const_spec((1, D_MODEL)),
                  pl.BlockSpec(wgu.shape, lambda i: (0, 0), **single),
                  pl.BlockSpec(wd.shape, lambda i: (0, 0), **single),
                  _const_spec((1, D_MODEL))],
        out_specs=xs,
        out_shape=jax.ShapeDtypeStruct((n, D_MODEL), F32),
        input_output_aliases={0: 0},
        compiler_params=_params(("arbitrary",)),
        name="swiglu",
    )(x2, gpre, wgu, wd, gpost)


def _pad_rows(w, offset):
    return jnp.zeros((LORA_PAD, w.shape[1]), w.dtype).at[offset:offset + w.shape[0]].set(w)


def _split_w_in(w_in):
    wc = w_in[:, :2 * D_CONV]
    wr = w_in[:, 2 * D_CONV:]
    wr = jnp.pad(wr, ((0, 0), (0, RWKV_COLS_PAD - wr.shape[1])))
    return wc.astype(BF16), wr.astype(BF16)


def kernel(x, mem, mem_norm_g, norm_gains, w_in_first, w_in_rest, mu_first, mu_rest, conv_w, conv_b, conv_ln_g, conv_ln_b, w0, w_up, a0, a_up, g_up, v0, v_up, k_k, k_a, r_k, lnx_g, lnx_b, w_out, wq, wkv, wo, w_gu, w_down):
    b, t, d = x.shape
    depth = norm_gains.shape[0]
    n = b * t
    assert d == D_MODEL and t % CHUNK == 0
    tm = min(512, n)
    tt = min(256, t)
    tq = min(512, t)
    assert n % tm == 0 and t % tt == 0 and t % tq == 0

    row = lambda a: a.reshape(1, -1).astype(F32)
    hid = jnp.arange(D_RWKV) // HEAD
    head_sum = (hid[:, None] == hid[None, :]).astype(BF16)
    ltri = (jnp.arange(CHUNK)[:, None] >= jnp.arange(CHUNK)[None, :]).astype(BF16)

    kv_all = _mem_kv(mem.reshape(b * mem.shape[1], d), row(mem_norm_g), wkv.astype(BF16))
    kv_all = kv_all.reshape(depth, b, mem.shape[1], 2 * d)

    x2 = x.reshape(n, d)
    vfirst = None
    for i in range(depth):
        g = norm_gains[i]
        if i == 0:
            wc, wr = _split_w_in(w_in_first)
            mu = mu_first
        else:
            wc, wr = _split_w_in(w_in_rest[i - 1])
            mu = mu_rest[i - 1]
        mu = jnp.pad(mu, (0, RWKV_COLS_PAD - mu.shape[0]))
        prm = {
            "mu": row(mu), "w0": row(w0[i]), "a0": row(a0[i]),
            "w_up": _pad_rows(w_up[i], 0).astype(BF16),
            "a_up": _pad_rows(a_up[i], w_up.shape[1]).astype(BF16),
            "g_up": g_up[i].astype(BF16),
            "k_k": row(k_k[i]), "k_a": row(k_a[i]), "r_k": row(r_k[i]),
            "lnx_g": row(lnx_g[i]), "lnx_b": row(lnx_b[i]),
            "head_sum": head_sum, "ltri": ltri,
        }
        if i > 0:
            prm["v0"] = row(v0[i - 1])
            prm["v_up"] = _pad_rows(v_up[i - 1], 0).astype(BF16)

        pc, pr = _in_proj(x2, row(g[0]), wc, wr, tm)
        yc = _conv(pc.reshape(b, t, 2 * D_CONV), conv_w[i], row(conv_b[i]), row(conv_ln_g[i]),
                   row(conv_ln_b[i]), tt)
        yr, vfirst = _rwkv(pr.reshape(b, t, RWKV_COLS_PAD), vfirst, prm, tt)
        w_o = w_out[i].astype(BF16)
        x2 = _out_proj(yc.reshape(n, D_CONV), yr.reshape(n, D_RWKV), x2, w_o[:D_CONV], w_o[D_CONV:],
                       row(g[1]), tm)
        x2 = _xattn(x2.reshape(b, t, d), kv_all[i], row(g[2]), wq[i].astype(BF16), wo[i].astype(BF16),
                    row(g[3]), tq).reshape(n, d)
        x2 = _swiglu(x2, row(g[4]), w_gu[i].astype(BF16), w_down[i].astype(BF16), row(g[5]), tm)
    return x2.reshape(b, t, d)
```

```python
import functools
import math

import jax
import jax.numpy as jnp
from jax import lax
from jax.experimental import pallas as pl
from jax.experimental.pallas import tpu as pltpu

F32 = jnp.float32
BF16 = jnp.bfloat16

RMS_EPS = 1e-6
LN_EPS = 1e-5
GN_EPS = 64e-5
DECAY_SCALE = math.exp(-0.5)

D_MODEL = 1024
D_CONV = 512
D_RWKV = 512
HEAD = 64
CONV_WIDTH = 31
XA_HEADS = 4
XA_HEAD_DIM = D_MODEL // XA_HEADS
LORA_PAD = 128
RWKV_COLS_PAD = 3 * D_RWKV + 3 * LORA_PAD

CHUNK = 64
GROUP = 4 * HEAD
CONV_HIST = 32
SUBLANES = 8

VMEM_LIMIT_BYTES = 56 * 1024 * 1024


def _params(sem):
    return pltpu.CompilerParams(dimension_semantics=sem, vmem_limit_bytes=VMEM_LIMIT_BYTES)


def _dot(a, b):
    return jnp.dot(a, b, preferred_element_type=F32)


def _dot_nt(a, b):
    return lax.dot_general(a, b, (((1,), (1,)), ((), ())), preferred_element_type=F32)


def _dot_tn(a, b):
    return lax.dot_general(a, b, (((0,), (0,)), ((), ())), preferred_element_type=F32)


def _rms(x, g):
    return x * lax.rsqrt(jnp.mean(x * x, axis=-1, keepdims=True) + RMS_EPS) * g


def _split_dot_rhs(lhs_bf16, x):
    hi = x.astype(BF16)
    lo = (x - hi.astype(F32)).astype(BF16)
    return _dot(lhs_bf16, hi) + _dot(lhs_bf16, lo)


def _const_spec(shape):
    nd = len(shape)
    return pl.BlockSpec(shape, lambda *_: (0,) * nd)


def _in_proj_kernel(x_ref, g_ref, wc_ref, wr_ref, pc_ref, pr_ref):
    h = _rms(x_ref[...], g_ref[...]).astype(BF16)
    pc_ref[...] = _dot(h, wc_ref[...])
    pr_ref[...] = _dot(h, wr_ref[...])


def _in_proj(x2, g, wc, wr, tm):
    n = x2.shape[0]
    return pl.pallas_call(
        _in_proj_kernel,
        grid=(n // tm,),
        in_specs=[
            pl.BlockSpec((tm, D_MODEL), lambda i: (i, 0)),
            _const_spec((1, D_MODEL)),
            _const_spec(wc.shape),
            _const_spec(wr.shape),
        ],
        out_specs=[
            pl.BlockSpec((tm, 2 * D_CONV), lambda i: (i, 0)),
            pl.BlockSpec((tm, RWKV_COLS_PAD), lambda i: (i, 0)),
        ],
        out_shape=[
            jax.ShapeDtypeStruct((n, 2 * D_CONV), F32),
            jax.ShapeDtypeStruct((n, RWKV_COLS_PAD), F32),
        ],
        compiler_params=_params(("arbitrary",)),
        name="in_proj",
    )(x2, g, wc, wr)


def _conv_kernel(tt, pc_ref, cw_ref, cb_ref, lg_ref, lb_ref, o_ref, hbuf, shifted):
    t = pl.program_id(1)

    @pl.when(t == 0)
    def _():
        hbuf[0:CONV_HIST, :] = jnp.zeros((CONV_HIST, D_CONV), F32)

    @pl.when(t > 0)
    def _():
        hbuf[0:CONV_HIST, :] = hbuf[tt:tt + CONV_HIST, :]

    u = pc_ref[...]
    hbuf[CONV_HIST:, :] = u[:, :D_CONV] * jax.nn.sigmoid(u[:, D_CONV:])

    rows = tt + CONV_HIST
    hval = hbuf[...]
    for r in range(1, SUBLANES):
        shifted[r - 1] = pltpu.roll(hval, rows - r, 0)

    rb = 32
    base = CONV_HIST - (CONV_WIDTH - 1)
    for r0 in range(0, tt, rb):
        acc = jnp.broadcast_to(cb_ref[...], (rb, D_CONV))
        for j in range(CONV_WIDTH):
            q, r = divmod(base + j, SUBLANES)
            lo = r0 + q * SUBLANES
            tap = hbuf[lo:lo + rb, :] if r == 0 else shifted[r - 1, lo:lo + rb, :]
            acc = acc + tap * cw_ref[j:j + 1, :]
        mean = jnp.mean(acc, axis=-1, keepdims=True)
        d = acc - mean
        var = jnp.mean(d * d, axis=-1, keepdims=True)
        y = d * lax.rsqrt(var + LN_EPS) * lg_ref[...] + lb_ref[...]
        o_ref[r0:r0 + rb, :] = (y * jax.nn.sigmoid(y)).astype(o_ref.dtype)


def _conv(pc3, cw, cb, lg, lb, tt):
    b, t, _ = pc3.shape
    return pl.pallas_call(
        functools.partial(_conv_kernel, tt),
        grid=(b, t // tt),
        in_specs=[
            pl.BlockSpec((None, tt, 2 * D_CONV), lambda i, j: (i, j, 0)),
            _const_spec(cw.shape),
            _const_spec((1, D_CONV)),
            _const_spec((1, D_CONV)),
            _const_spec((1, D_CONV)),
        ],
        out_specs=pl.BlockSpec((None, tt, D_CONV), lambda i, j: (i, j, 0)),
        out_shape=jax.ShapeDtypeStruct((b, t, D_CONV), BF16),
        scratch_shapes=[pltpu.VMEM((tt + CONV_HIST, D_CONV), F32),
                        pltpu.VMEM((SUBLANES - 1, tt + CONV_HIST, D_CONV), F32)],
        compiler_params=_params(("arbitrary", "arbitrary")),
        name="conformer_conv",
    )(pc3, cw, cb, lg, lb)


def _blk(z, head_mask):
    return jnp.concatenate([z] * (GROUP // CHUNK), axis=0) * head_mask


def _chunk_precompute(a_t, r_t, b_t, k_t, v, head_mask, strict, incl, eye):
    c = CHUNK
    n = range(len(a_t))
    a_all = [_dot_nt(jnp.concatenate([a_t[i], r_t[i]], axis=0),
                     jnp.concatenate([_blk(b_t[i], head_mask), _blk(k_t[i], head_mask)], axis=0))
             for i in n]
    a_ab = [jnp.where(strict, a_all[i][:c, :GROUP], 0.0) for i in n]
    a_ak = [jnp.where(strict, a_all[i][:c, GROUP:], 0.0).astype(BF16) for i in n]
    a_rb = [jnp.where(incl, a_all[i][c:, :GROUP], 0.0).astype(BF16) for i in n]
    a_rk = [jnp.where(incl, a_all[i][c:, GROUP:], 0.0).astype(BF16) for i in n]

    q = [jnp.where(eye, 1.0, 0.0) + a_ab[i] for i in n]
    ab = [a_ab[i].astype(BF16) for i in n]
    ak = [_dot(ab[i], _blk(ab[i], head_mask)).astype(BF16) for i in n]
    for _ in range(int(math.log2(c)) - 2):
        qa = [_dot(jnp.concatenate([q[i].astype(BF16), ak[i]], axis=0), _blk(ak[i], head_mask))
              for i in n]
        q = [q[i] + qa[i][:c] for i in n]
        ak = [qa[i][c:].astype(BF16) for i in n]
    inv = [q[i] + _dot(q[i].astype(BF16), _blk(ak[i], head_mask)) for i in n]

    z = [_dot(a_ak[i], _blk(v[i], head_mask)).astype(BF16) for i in n]
    wu = [_dot(inv[i].astype(BF16),
               jnp.concatenate([_blk(a_t[i], head_mask), _blk(z[i], head_mask)], axis=1))
          for i in n]
    return ([wu[i][:, :GROUP].astype(BF16) for i in n], [wu[i][:, GROUP:] for i in n], a_rb, a_rk)


def _chunk_state_step(w, r_t, u0, a_rb, a_rk, v, b_p, k_p, gcol, st, head_mask, same_head):
    c = CHUNK
    n = range(len(w))
    wr = [_dot(jnp.concatenate([w[i], r_t[i]], axis=0), st[i].astype(BF16)) for i in n]
    u = [(wr[i][:c] + u0[i]).astype(BF16) for i in n]
    upd = [_dot_tn(jnp.concatenate([b_p[i], k_p[i]], axis=0), jnp.concatenate([u[i], v[i]], axis=0))
           for i in n]
    st_new = [st[i] * gcol[i] + jnp.where(same_head, upd[i], 0.0) for i in n]
    y = [wr[i][c:] + _dot(jnp.concatenate([a_rb[i], a_rk[i]], axis=1),
                          jnp.concatenate([_blk(u[i], head_mask), _blk(v[i], head_mask)], axis=0))
         for i in n]
    return y, st_new


def _rwkv_kernel(tt, has_vfirst, *refs):
    if has_vfirst:
        (pr_ref, vf_ref, mu_ref, w0_ref, wup_ref, a0_ref, aup_ref, gup_ref, v0_ref, vup_ref,
         kk_ref, ka_ref, rk_ref, lng_ref, lnb_ref, ltile_ref, hm_ref,
         y_ref,
         carry, state, gcol_s, at_s, rt_s, bt_s, kt_s, bp_s, kp_s, vb_s, w_s, arb_s, ark_s,
         u0_s, bonus_s, g_s, y_s) = refs
    else:
        (pr_ref, mu_ref, w0_ref, wup_ref, a0_ref, aup_ref, gup_ref,
         kk_ref, ka_ref, rk_ref, lng_ref, lnb_ref, ltile_ref, hm_ref,
         y_ref, vfo_ref,
         carry, state, gcol_s, at_s, rt_s, bt_s, kt_s, bp_s, kp_s, vb_s, w_s, arb_s, ark_s,
         u0_s, bonus_s, g_s, y_s) = refs

    @pl.when(pl.program_id(1) == 0)
    def _():
        carry[...] = jnp.zeros_like(carry)
        state[...] = jnp.zeros_like(state)

    head_mask = hm_ref[...]
    n_groups = D_RWKV // GROUP

    def head_sum(x):
        xb = x.astype(BF16)
        return jnp.concatenate(
            [_dot(xb[:, g * GROUP:(g + 1) * GROUP], head_mask) for g in range(n_groups)], axis=1)

    p = pr_ref[...]
    rolled = pltpu.roll(p, 1, 0)
    row = lax.broadcasted_iota(jnp.int32, p.shape, 0)
    prev = jnp.where(row == 0, carry[0:1, :], rolled)
    carry[0:1, :] = p[tt - 1:tt, :]
    ps = p + mu_ref[...] * (prev - p)

    r = ps[:, 0:D_RWKV]
    k = ps[:, D_RWKV:2 * D_RWKV]
    v = ps[:, 2 * D_RWKV:3 * D_RWKV]
    o = 3 * D_RWKV
    wa_lo = ps[:, o:o + LORA_PAD]
    g_lo = ps[:, o + LORA_PAD:o + 2 * LORA_PAD]
    lw = -DECAY_SCALE * jax.nn.sigmoid(w0_ref[...] + _dot(jnp.tanh(wa_lo).astype(BF16), wup_ref[...]))
    eta = jax.nn.sigmoid(a0_ref[...] + _dot(wa_lo.astype(BF16), aup_ref[...]))
    g_s[...] = _dot(jax.nn.sigmoid(g_lo).astype(BF16), gup_ref[...])
    if has_vfirst:
        v_lo = ps[:, o + 2 * LORA_PAD:o + 3 * LORA_PAD]
        v = v + (vf_ref[...] - v) * jax.nn.sigmoid(v0_ref[...] + _dot(v_lo.astype(BF16), vup_ref[...]))
    else:
        vfo_ref[...] = v
    kk = k * kk_ref[...]
    kk = kk * lax.rsqrt(jnp.maximum(head_sum(kk * kk), 1e-24))
    k = k * (1.0 + (eta - 1.0) * ka_ref[...])
    bonus_s[...] = head_sum(r * k * rk_ref[...]) * v
    bb = -(kk * eta)

    cin = _split_dot_rhs(ltile_ref[...], lw)
    ctot = jnp.concatenate(
        [jnp.broadcast_to(cin[c0 + CHUNK - 1:c0 + CHUNK, :], (CHUNK, D_RWKV))
         for c0 in range(0, tt, CHUNK)], axis=0)
    at_s[...] = (kk * jnp.exp(cin - lw)).astype(BF16)
    rt_s[...] = (r * jnp.exp(cin)).astype(BF16)
    ginv = jnp.exp(-cin)
    bt_s[...] = (bb * ginv).astype(BF16)
    kt_s[...] = (k * ginv).astype(BF16)
    gl = jnp.exp(ctot - cin)
    bp_s[...] = (bb * gl).astype(BF16)
    kp_s[...] = (k * gl).astype(BF16)
    vb_s[...] = v.astype(BF16)
    gc = jnp.exp(ctot)

    gi = lax.broadcasted_iota(jnp.int32, (GROUP, GROUP), 0) // HEAD
    gj = lax.broadcasted_iota(jnp.int32, (GROUP, GROUP), 1) // HEAD
    same_head = gi == gj
    ti = lax.broadcasted_iota(jnp.int32, (CHUNK, GROUP), 0)
    si = lax.broadcasted_iota(jnp.int32, (CHUNK, GROUP), 1) % CHUNK
    strict = si < ti
    incl = si <= ti
    eye = si == ti
    bf16_rows = 16
    sel = lax.broadcasted_iota(jnp.int32, (bf16_rows, GROUP), 0)
    ones_rows = jnp.ones((bf16_rows, GROUP), BF16)
    bodies = [(slice(ci * CHUNK, (ci + 1) * CHUNK), slice(grp * GROUP, (grp + 1) * GROUP))
              for ci in range(tt // CHUNK) for grp in range(n_groups)]
    w, u0, a_rb, a_rk = _chunk_precompute(
        [at_s[b] for b in bodies], [rt_s[b] for b in bodies], [bt_s[b] for b in bodies],
        [kt_s[b] for b in bodies], [vb_s[b] for b in bodies], head_mask, strict, incl, eye)
    for i, b in enumerate(bodies):
        w_s[b] = w[i]
        u0_s[b] = u0[i]
        arb_s[b] = a_rb[i]
        ark_s[b] = a_rk[i]
        g_row = gc[b[0].start:b[0].start + 1, b[1]]
        g_hi = g_row.astype(BF16).astype(F32)
        split = jnp.where(sel == 0, g_hi, jnp.where(sel == 1, g_row - g_hi, 0.0)).astype(BF16)
        gcol_s[i] = _dot_tn(split, ones_rows)

    def chunk_step(ci, carry_):
        r0 = pl.multiple_of(ci * CHUNK, CHUNK)
        rows = pl.ds(r0, CHUNK)
        groups = [(rows, slice(grp * GROUP, (grp + 1) * GROUP)) for grp in range(n_groups)]
        take = lambda ref: [ref[g] for g in groups]
        y, st = _chunk_state_step(
            take(w_s), take(rt_s), take(u0_s), take(arb_s), take(ark_s), take(vb_s), take(bp_s),
            take(kp_s), [gcol_s[ci * n_groups + grp] for grp in range(n_groups)],
            [state[grp] for grp in range(n_groups)], head_mask, same_head)
        for grp, g in enumerate(groups):
            y_s[g] = y[grp]
            state[grp] = st[grp]
        return carry_

    lax.fori_loop(0, tt // CHUNK, chunk_step, 0)

    y = y_s[...]
    mean = head_sum(y) * (1.0 / HEAD)
    d = y - mean
    var = head_sum(d * d) * (1.0 / HEAD)
    yn = d * lax.rsqrt(var + GN_EPS) * lng_ref[...] + lnb_ref[...]
    y_ref[...] = ((yn + bonus_s[...]) * g_s[...]).astype(y_ref.dtype)


def _rwkv(pr3, vfirst3, prm, tt):
    b, t, _ = pr3.shape
    has_vfirst = vfirst3 is not None
    tile = lambda w: pl.BlockSpec((None, tt, w), lambda i, j: (i, j, 0))
    row = _const_spec((1, D_RWKV))
    lora = _const_spec((LORA_PAD, D_RWKV))
    in_specs = [tile(RWKV_COLS_PAD)]
    args = [pr3]
    if has_vfirst:
        in_specs.append(tile(D_RWKV))
        args.append(vfirst3)
    in_specs += [_const_spec((1, RWKV_COLS_PAD)), row, lora, row, lora, lora]
    args += [prm["mu"], prm["w0"], prm["w_up"], prm["a0"], prm["a_up"], prm["g_up"]]
    if has_vfirst:
        in_specs += [row, lora]
        args += [prm["v0"], prm["v_up"]]
    in_specs += [row, row, row, row, row, _const_spec((tt, tt)), _const_spec((GROUP, GROUP))]
    args += [prm["k_k"], prm["k_a"], prm["r_k"], prm["lnx_g"], prm["lnx_b"],
             prm["ltile"], prm["head_mask"]]
    out_specs = [tile(D_RWKV)]
    out_shape = [jax.ShapeDtypeStruct((b, t, D_RWKV), BF16)]
    if not has_vfirst:
        out_specs.append(tile(D_RWKV))
        out_shape.append(jax.ShapeDtypeStruct((b, t, D_RWKV), F32))
    n_groups = D_RWKV // GROUP
    work_bf16 = pltpu.VMEM((tt, D_RWKV), BF16)
    work_f32 = pltpu.VMEM((tt, D_RWKV), F32)
    outs = pl.pallas_call(
        functools.partial(_rwkv_kernel, tt, has_vfirst),
        grid=(b, t // tt),
        in_specs=in_specs,
        out_specs=out_specs,
        out_shape=out_shape,
        scratch_shapes=[pltpu.VMEM((8, RWKV_COLS_PAD), F32),
                        pltpu.VMEM((n_groups, GROUP, GROUP), F32),
                        pltpu.VMEM((tt // CHUNK * n_groups, GROUP, GROUP), F32)]
                       + [work_bf16] * 10 + [work_f32] * 4,
        compiler_params=_params(("arbitrary", "arbitrary")),
        name="rwkv7_mix",
    )(*args)
    if has_vfirst:
        return outs[0], vfirst3
    return outs[0], outs[1]


def _out_proj_kernel(yc_ref, yr_ref, x_ref, wt_ref, wb_ref, g_ref, o_ref):
    y = _dot(yc_ref[...], wt_ref[...]) + _dot(yr_ref[...], wb_ref[...])
    o_ref[...] = x_ref[...] + _rms(y, g_ref[...])


def _out_proj(yc2, yr2, x2, wt, wb, g, tm):
    n = x2.shape[0]
    half = pl.BlockSpec((tm, D_CONV), lambda i: (i, 0))
    full = pl.BlockSpec((tm, D_MODEL), lambda i: (i, 0))
    return pl.pallas_call(
        _out_proj_kernel,
        grid=(n // tm,),
        in_specs=[half, half, full, _const_spec(wt.shape), _const_spec(wb.shape), _const_spec((1, D_MODEL))],
        out_specs=full,
        out_shape=jax.ShapeDtypeStruct((n, D_MODEL), F32),
        input_output_aliases={2: 0},
        compiler_params=_params(("arbitrary",)),
        name="mixer_out_proj",
    )(yc2, yr2, x2, wt, wb, g)


def _mem_kv_kernel(mem_ref, g_ref, wkv_ref, kv_ref):
    mn = _rms(mem_ref[...], g_ref[...]).astype(BF16)
    kv_ref[...] = _dot(mn, wkv_ref[...]).astype(kv_ref.dtype)


def _mem_kv(mem2, g, wkv):
    layers = wkv.shape[0]
    m = mem2.shape[0]
    return pl.pallas_call(
        _mem_kv_kernel,
        grid=(layers,),
        in_specs=[_const_spec(mem2.shape), _const_spec((1, D_MODEL)),
                  pl.BlockSpec((None, D_MODEL, 2 * D_MODEL), lambda l: (l, 0, 0))],
        out_specs=pl.BlockSpec((None, m, 2 * D_MODEL), lambda l: (l, 0, 0)),
        out_shape=jax.ShapeDtypeStruct((layers, m, 2 * D_MODEL), BF16),
        compiler_params=_params(("arbitrary",)),
        name="memory_kv",
    )(mem2, g, wkv)


def _xattn_kernel(x_ref, kv_ref, gpre_ref, wq_ref, wo_ref, gpost_ref, o_ref):
    x = x_ref[...]
    h = _rms(x, gpre_ref[...]).astype(BF16)
    q = (_dot(h, wq_ref[...]) * (XA_HEAD_DIM ** -0.5)).astype(BF16)
    outs = []
    for hd in range(XA_HEADS):
        lo = hd * XA_HEAD_DIM
        s = _dot_nt(q[:, lo:lo + XA_HEAD_DIM], kv_ref[:, lo:lo + XA_HEAD_DIM])
        e = jnp.exp(s - jnp.max(s, axis=-1, keepdims=True))
        pr = (e / jnp.sum(e, axis=-1, keepdims=True)).astype(BF16)
        outs.append(_dot(pr, kv_ref[:, D_MODEL + lo:D_MODEL + lo + XA_HEAD_DIM]))
    o = jnp.concatenate(outs, axis=-1).astype(BF16)
    o_ref[...] = x + _rms(_dot(o, wo_ref[...]), gpost_ref[...])


def _xattn(x3, kv3, gpre, wq, wo, gpost, tq):
    b, t, _ = x3.shape
    m = kv3.shape[1]
    xs = pl.BlockSpec((None, tq, D_MODEL), lambda i, j: (i, j, 0))
    return pl.pallas_call(
        _xattn_kernel,
        grid=(b, t // tq),
        in_specs=[xs, pl.BlockSpec((None, m, 2 * D_MODEL), lambda i, j: (i, 0, 0)),
                  _const_spec((1, D_MODEL)), _const_spec(wq.shape), _const_spec(wo.shape),
                  _const_spec((1, D_MODEL))],
        out_specs=xs,
        out_shape=jax.ShapeDtypeStruct(x3.shape, F32),
        input_output_aliases={0: 0},
        compiler_params=_params(("arbitrary", "arbitrary")),
        name="memory_xattn",
    )(x3, kv3, gpre, wq, wo, gpost)


def _swiglu_kernel(d_ff, ff_chunk, x_ref, gpre_ref, wgu_ref, wd_ref, gpost_ref, o_ref):
    x = x_ref[...]
    h = _rms(x, gpre_ref[...]).astype(BF16)
    acc = jnp.zeros(x.shape, F32)
    for c0 in range(0, d_ff, ff_chunk):
        gate = _dot(h, wgu_ref[:, c0:c0 + ff_chunk])
        up = _dot(h, wgu_ref[:, d_ff + c0:d_ff + c0 + ff_chunk])
        act = (gate * jax.nn.sigmoid(gate) * up).astype(BF16)
        acc = acc + _dot(act, wd_ref[c0:c0 + ff_chunk, :])
    o_ref[...] = x + _rms(acc, gpost_ref[...])


def _swiglu(x2, gpre, wgu, wd, gpost, tm):
    n = x2.shape[0]
    d_ff = wd.shape[0]
    ff_chunk = next(c for c in (1408, 1024, 512, 256, 128) if d_ff % c == 0)
    xs = pl.BlockSpec((tm, D_MODEL), lambda i: (i, 0))
    single = dict(pipeline_mode=pl.Buffered(1))
    return pl.pallas_call(
        functools.partial(_swiglu_kernel, d_ff, ff_chunk),
        grid=(n // tm,),
        in_specs=[xs, _const_spec((1, D_MODEL)),
                  pl.BlockSpec(wgu.shape, lambda i: (0, 0), **single),
                  pl.BlockSpec(wd.shape, lambda i: (0, 0), **single),
                  _const_spec((1, D_MODEL))],
        out_specs=xs,
        out_shape=jax.ShapeDtypeStruct((n, D_MODEL), F32),
        input_output_aliases={0: 0},
        compiler_params=_params(("arbitrary",)),
        name="swiglu",
    )(x2, gpre, wgu, wd, gpost)


def _pad_rows(w, offset):
    return jnp.zeros((LORA_PAD, w.shape[1]), w.dtype).at[offset:offset + w.shape[0]].set(w)


def _split_w_in(w_in):
    wc = w_in[:, :2 * D_CONV]
    wr = w_in[:, 2 * D_CONV:]
    wr = jnp.pad(wr, ((0, 0), (0, RWKV_COLS_PAD - wr.shape[1])))
    return wc.astype(BF16), wr.astype(BF16)


def kernel(x, mem, mem_norm_g, norm_gains, w_in_first, w_in_rest, mu_first, mu_rest, conv_w, conv_b, conv_ln_g, conv_ln_b, w0, w_up, a0, a_up, g_up, v0, v_up, k_k, k_a, r_k, lnx_g, lnx_b, w_out, wq, wkv, wo, w_gu, w_down):
    b, t, d = x.shape
    depth = norm_gains.shape[0]
    n = b * t
    assert d == D_MODEL and t % CHUNK == 0
    tm = min(512, n)
    tt = min(256, t)
    tq = min(512, t)
    assert n % tm == 0 and t % tt == 0 and t % tq == 0

    row = lambda a: a.reshape(1, -1).astype(F32)
    tpos = jnp.arange(tt)
    same_chunk = (tpos[:, None] // CHUNK) == (tpos[None, :] // CHUNK)
    ltile = (same_chunk & (tpos[:, None] >= tpos[None, :])).astype(BF16)
    gid = jnp.arange(GROUP) // HEAD
    head_mask = (gid[:, None] == gid[None, :]).astype(BF16)

    kv_all = _mem_kv(mem.reshape(b * mem.shape[1], d), row(mem_norm_g), wkv.astype(BF16))
    kv_all = kv_all.reshape(depth, b, mem.shape[1], 2 * d)

    x2 = x.reshape(n, d)
    vfirst = None
    for i in range(depth):
        g = norm_gains[i]
        if i == 0:
            wc, wr = _split_w_in(w_in_first)
            mu = mu_first
        else:
            wc, wr = _split_w_in(w_in_rest[i - 1])
            mu = mu_rest[i - 1]
        mu = jnp.pad(mu, (0, RWKV_COLS_PAD - mu.shape[0]))
        prm = {
            "mu": row(mu), "w0": row(w0[i]), "a0": row(a0[i]),
            "w_up": _pad_rows(w_up[i], 0).astype(BF16),
            "a_up": _pad_rows(a_up[i], w_up.shape[1]).astype(BF16),
            "g_up": g_up[i].astype(BF16),
            "k_k": row(k_k[i]), "k_a": row(k_a[i]), "r_k": row(r_k[i]),
            "lnx_g": row(lnx_g[i]), "lnx_b": row(lnx_b[i]),
            "ltile": ltile, "head_mask": head_mask,
        }
        if i > 0:
            prm["v0"] = row(v0[i - 1])
            prm["v_up"] = _pad_rows(v_up[i - 1], 0).astype(BF16)

        pc, pr = _in_proj(x2, row(g[0]), wc, wr, tm)
        yc = _conv(pc.reshape(b, t, 2 * D_CONV), conv_w[i], row(conv_b[i]), row(conv_ln_g[i]),
                   row(conv_ln_b[i]), tt)
        yr, vfirst = _rwkv(pr.reshape(b, t, RWKV_COLS_PAD), vfirst, prm, tt)
        w_o = w_out[i].astype(BF16)
        x2 = _out_proj(yc.reshape(n, D_CONV), yr.reshape(n, D_RWKV), x2, w_o[:D_CONV], w_o[D_CONV:],
                       row(g[1]), tm)
        x2 = _xattn(x2.reshape(b, t, d), kv_all[i], row(g[2]), wq[i].astype(BF16), wo[i].astype(BF16),
                    row(g[3]), tq).reshape(n, d)
        x2 = _swiglu(x2, row(g[4]), w_gu[i].astype(BF16), w_down[i].astype(BF16), row(g[5]), tm)
    return x2.reshape(b, t, d)
```

```python
import functools
import math

import jax
import jax.numpy as jnp
from jax import lax
from jax.experimental import pallas as pl
from jax.experimental.pallas import tpu as pltpu

F32 = jnp.float32
BF16 = jnp.bfloat16

RMS_EPS = 1e-6
LN_EPS = 1e-5
GN_EPS = 64e-5
DECAY_SCALE = math.exp(-0.5)

D_MODEL = 1024
D_CONV = 512
D_RWKV = 512
HEAD = 64
CONV_WIDTH = 31
XA_HEADS = 4
XA_HEAD_DIM = D_MODEL // XA_HEADS
LORA_PAD = 128
RWKV_COLS_PAD = 3 * D_RWKV + 3 * LORA_PAD

CHUNK = 64
GROUP = 4 * HEAD
CONV_HIST = 32
SUBLANES = 8

VMEM_LIMIT_BYTES = 56 * 1024 * 1024


def _params(sem):
    return pltpu.CompilerParams(dimension_semantics=sem, vmem_limit_bytes=VMEM_LIMIT_BYTES)


def _dot(a, b):
    return jnp.dot(a, b, preferred_element_type=F32)


def _dot_nt(a, b):
    return lax.dot_general(a, b, (((1,), (1,)), ((), ())), preferred_element_type=F32)


def _dot_tn(a, b):
    return lax.dot_general(a, b, (((0,), (0,)), ((), ())), preferred_element_type=F32)


def _rms(x, g):
    return x * lax.rsqrt(jnp.mean(x * x, axis=-1, keepdims=True) + RMS_EPS) * g


def _split_dot_rhs(lhs_bf16, x):
    hi = x.astype(BF16)
    lo = (x - hi.astype(F32)).astype(BF16)
    return _dot(lhs_bf16, hi) + _dot(lhs_bf16, lo)


def _const_spec(shape):
    nd = len(shape)
    return pl.BlockSpec(shape, lambda *_: (0,) * nd)


def _in_proj_conv_kernel(tt, x_ref, g_ref, wc_ref, wr_ref, cw_ref, cb_ref, lg_ref, lb_ref,
                         yc_ref, pr_ref, h_s, hbuf, shifted):
    first_tile = pl.program_id(1) == 0

    @pl.when(first_tile)
    def _():
        hbuf[0:CONV_HIST, :] = jnp.zeros((CONV_HIST, D_CONV), F32)

    @pl.when(jnp.logical_not(first_tile))
    def _():
        hbuf[0:CONV_HIST, :] = hbuf[tt:tt + CONV_HIST, :]

    mb = min(128, tt)
    for r0 in range(0, tt, mb):
        h_s[r0:r0 + mb, :] = _rms(x_ref[r0:r0 + mb, :], g_ref[...]).astype(BF16)
    nb = 256
    for r0 in range(0, tt, mb):
        for c0 in range(0, D_CONV, nb):
            val = _dot(h_s[r0:r0 + mb, :], wc_ref[:, c0:c0 + nb])
            gate = _dot(h_s[r0:r0 + mb, :], wc_ref[:, D_CONV + c0:D_CONV + c0 + nb])
            hbuf[CONV_HIST + r0:CONV_HIST + r0 + mb, c0:c0 + nb] = val * jax.nn.sigmoid(gate)
    for r0 in range(0, tt, mb):
        for c0 in range(0, RWKV_COLS_PAD, nb):
            c1 = min(c0 + nb, RWKV_COLS_PAD)
            pr_ref[r0:r0 + mb, c0:c1] = _dot(h_s[r0:r0 + mb, :], wr_ref[:, c0:c1]).astype(pr_ref.dtype)

    rb = 32
    base = CONV_HIST - (CONV_WIDTH - 1)
    last_q = (base + CONV_WIDTH - 1) // SUBLANES - 1
    n_shifted = tt - rb + last_q * SUBLANES + rb
    for r in range(1, SUBLANES):
        for b0 in range(0, n_shifted, rb):
            nb = min(rb, n_shifted - b0)
            shifted[r - 1, b0:b0 + nb, :] = hbuf[b0 + r:b0 + r + nb, :]
    for r0 in range(0, tt, rb):
        acc = jnp.broadcast_to(cb_ref[...], (rb, D_CONV))
        for j in range(CONV_WIDTH):
            q, r = divmod(base + j, SUBLANES)
            lo = r0 + q * SUBLANES
            tap = hbuf[lo:lo + rb, :] if r == 0 else shifted[r - 1, lo:lo + rb, :]
            acc = acc + tap * cw_ref[j:j + 1, :]
        mean = jnp.mean(acc, axis=-1, keepdims=True)
        d = acc - mean
        var = jnp.mean(d * d, axis=-1, keepdims=True)
        y = d * lax.rsqrt(var + LN_EPS) * lg_ref[...] + lb_ref[...]
        yc_ref[r0:r0 + rb, :] = (y * jax.nn.sigmoid(y)).astype(yc_ref.dtype)


def _in_proj_conv(x3, g, wc, wr, cw, cb, lg, lb, tt):
    b, t, _ = x3.shape
    tile = lambda w: pl.BlockSpec((None, tt, w), lambda i, j: (i, j, 0))
    crow = _const_spec((1, D_CONV))
    return pl.pallas_call(
        functools.partial(_in_proj_conv_kernel, tt),
        grid=(b, t // tt),
        in_specs=[tile(D_MODEL), _const_spec((1, D_MODEL)), _const_spec(wc.shape),
                  _const_spec(wr.shape), _const_spec(cw.shape), crow, crow, crow],
        out_specs=[tile(D_CONV), tile(RWKV_COLS_PAD)],
        out_shape=[jax.ShapeDtypeStruct((b, t, D_CONV), BF16),
                   jax.ShapeDtypeStruct((b, t, RWKV_COLS_PAD), BF16)],
        scratch_shapes=[pltpu.VMEM((tt, D_MODEL), BF16),
                        pltpu.VMEM((tt + CONV_HIST, D_CONV), F32),
                        pltpu.VMEM((SUBLANES - 1, tt + CONV_HIST, D_CONV), F32)],
        compiler_params=_params(("arbitrary", "arbitrary")),
        name="in_proj_conv",
    )(x3, g, wc, wr, cw, cb, lg, lb)


def _blk(z, head_mask):
    return jnp.concatenate([z] * (GROUP // CHUNK), axis=0) * head_mask


def _chunk_precompute(a_t, r_t, b_t, k_t, v, head_mask, strict, incl, eye):
    c = CHUNK
    n = range(len(a_t))
    a_all = [_dot_nt(jnp.concatenate([a_t[i], r_t[i]], axis=0),
                     jnp.concatenate([_blk(b_t[i], head_mask), _blk(k_t[i], head_mask)], axis=0))
             for i in n]
    a_ab = [jnp.where(strict, a_all[i][:c, :GROUP], 0.0) for i in n]
    a_ak = [jnp.where(strict, a_all[i][:c, GROUP:], 0.0).astype(BF16) for i in n]
    a_rb = [jnp.where(incl, a_all[i][c:, :GROUP], 0.0).astype(BF16) for i in n]
    a_rk = [jnp.where(incl, a_all[i][c:, GROUP:], 0.0).astype(BF16) for i in n]

    q = [jnp.where(eye, 1.0, 0.0) + a_ab[i] for i in n]
    ab = [a_ab[i].astype(BF16) for i in n]
    ak = [_dot(ab[i], _blk(ab[i], head_mask)).astype(BF16) for i in n]
    for _ in range(int(math.log2(c)) - 2):
        qa = [_dot(jnp.concatenate([q[i].astype(BF16), ak[i]], axis=0), _blk(ak[i], head_mask))
              for i in n]
        q = [q[i] + qa[i][:c] for i in n]
        ak = [qa[i][c:].astype(BF16) for i in n]
    inv = [q[i] + _dot(q[i].astype(BF16), _blk(ak[i], head_mask)) for i in n]

    z = [_dot(a_ak[i], _blk(v[i], head_mask)).astype(BF16) for i in n]
    wu = [_dot(inv[i].astype(BF16),
               jnp.concatenate([_blk(a_t[i], head_mask), _blk(z[i], head_mask)], axis=1))
          for i in n]
    return ([wu[i][:, :GROUP].astype(BF16) for i in n], [wu[i][:, GROUP:] for i in n], a_rb, a_rk)


def _chunk_state_step(w, r_t, u0, a_rb, a_rk, v, b_p, k_p, gcol, st, head_mask, same_head):
    c = CHUNK
    n = range(len(w))
    wr = [_dot(jnp.concatenate([w[i], r_t[i]], axis=0), st[i].astype(BF16)) for i in n]
    u = [(wr[i][:c] + u0[i]).astype(BF16) for i in n]
    upd = [_dot_tn(jnp.concatenate([b_p[i], k_p[i]], axis=0), jnp.concatenate([u[i], v[i]], axis=0))
           for i in n]
    st_new = [st[i] * gcol[i] + jnp.where(same_head, upd[i], 0.0) for i in n]
    y = [wr[i][c:] + _dot(jnp.concatenate([a_rb[i], a_rk[i]], axis=1),
                          jnp.concatenate([_blk(u[i], head_mask), _blk(v[i], head_mask)], axis=0))
         for i in n]
    return y, st_new


def _rwkv_kernel(tt, has_vfirst, *refs):
    if has_vfirst:
        (pr_ref, vf_ref, mu_ref, w0_ref, wup_ref, a0_ref, aup_ref, gup_ref, v0_ref, vup_ref,
         kk_ref, ka_ref, rk_ref, lng_ref, lnb_ref, ltile_ref, hm_ref,
         y_ref,
         carry, state, gcol_s, at_s, rt_s, bt_s, kt_s, bp_s, kp_s, vb_s, w_s, arb_s, ark_s,
         u0_s, bonus_s, g_s, y_s) = refs
    else:
        (pr_ref, mu_ref, w0_ref, wup_ref, a0_ref, aup_ref, gup_ref,
         kk_ref, ka_ref, rk_ref, lng_ref, lnb_ref, ltile_ref, hm_ref,
         y_ref, vfo_ref,
         carry, state, gcol_s, at_s, rt_s, bt_s, kt_s, bp_s, kp_s, vb_s, w_s, arb_s, ark_s,
         u0_s, bonus_s, g_s, y_s) = refs

    @pl.when(pl.program_id(1) == 0)
    def _():
        carry[...] = jnp.zeros_like(carry)
        state[...] = jnp.zeros_like(state)

    head_mask = hm_ref[...]
    n_groups = D_RWKV // GROUP

    def head_sum(x):
        xb = x.astype(BF16)
        return jnp.concatenate(
            [_dot(xb[:, g * GROUP:(g + 1) * GROUP], head_mask) for g in range(n_groups)], axis=1)

    p = pr_ref[...].astype(F32)
    rolled = pltpu.roll(p, 1, 0)
    row = lax.broadcasted_iota(jnp.int32, p.shape, 0)
    prev = jnp.where(row == 0, carry[0:1, :], rolled)
    carry[0:1, :] = p[tt - 1:tt, :]
    ps = p + mu_ref[...] * (prev - p)

    r = ps[:, 0:D_RWKV]
    k = ps[:, D_RWKV:2 * D_RWKV]
    v = ps[:, 2 * D_RWKV:3 * D_RWKV]
    o = 3 * D_RWKV
    wa_lo = ps[:, o:o + LORA_PAD]
    g_lo = ps[:, o + LORA_PAD:o + 2 * LORA_PAD]
    lw = -DECAY_SCALE * jax.nn.sigmoid(w0_ref[...] + _dot(jnp.tanh(wa_lo).astype(BF16), wup_ref[...]))
    eta = jax.nn.sigmoid(a0_ref[...] + _dot(wa_lo.astype(BF16), aup_ref[...]))
    g_s[...] = _dot(jax.nn.sigmoid(g_lo).astype(BF16), gup_ref[...])
    if has_vfirst:
        v_lo = ps[:, o + 2 * LORA_PAD:o + 3 * LORA_PAD]
        v = v + (vf_ref[...] - v) * jax.nn.sigmoid(v0_ref[...] + _dot(v_lo.astype(BF16), vup_ref[...]))
    else:
        vfo_ref[...] = v
    kk = k * kk_ref[...]
    kk = kk * lax.rsqrt(jnp.maximum(head_sum(kk * kk), 1e-24))
    k = k * (1.0 + (eta - 1.0) * ka_ref[...])
    bonus_s[...] = head_sum(r * k * rk_ref[...]) * v
    bb = -(kk * eta)

    cin = _split_dot_rhs(ltile_ref[...], lw)
    ctot = jnp.concatenate(
        [jnp.broadcast_to(cin[c0 + CHUNK - 1:c0 + CHUNK, :], (CHUNK, D_RWKV))
         for c0 in range(0, tt, CHUNK)], axis=0)
    at_s[...] = (kk * jnp.exp(cin - lw)).astype(BF16)
    rt_s[...] = (r * jnp.exp(cin)).astype(BF16)
    ginv = jnp.exp(-cin)
    bt_s[...] = (bb * ginv).astype(BF16)
    kt_s[...] = (k * ginv).astype(BF16)
    gl = jnp.exp(ctot - cin)
    bp_s[...] = (bb * gl).astype(BF16)
    kp_s[...] = (k * gl).astype(BF16)
    vb_s[...] = v.astype(BF16)
    gc = jnp.exp(ctot)

    gi = lax.broadcasted_iota(jnp.int32, (GROUP, GROUP), 0) // HEAD
    gj = lax.broadcasted_iota(jnp.int32, (GROUP, GROUP), 1) // HEAD
    same_head = gi == gj
    ti = lax.broadcasted_iota(jnp.int32, (CHUNK, GROUP), 0)
    si = lax.broadcasted_iota(jnp.int32, (CHUNK, GROUP), 1) % CHUNK
    strict = si < ti
    incl = si <= ti
    eye = si == ti
    bf16_rows = 16
    sel = lax.broadcasted_iota(jnp.int32, (bf16_rows, GROUP), 0)
    ones_rows = jnp.ones((bf16_rows, GROUP), BF16)
    bodies = [(slice(ci * CHUNK, (ci + 1) * CHUNK), slice(grp * GROUP, (grp + 1) * GROUP))
              for ci in range(tt // CHUNK) for grp in range(n_groups)]
    w, u0, a_rb, a_rk = _chunk_precompute(
        [at_s[b] for b in bodies], [rt_s[b] for b in bodies], [bt_s[b] for b in bodies],
        [kt_s[b] for b in bodies], [vb_s[b] for b in bodies], head_mask, strict, incl, eye)
    for i, b in enumerate(bodies):
        w_s[b] = w[i]
        u0_s[b] = u0[i]
        arb_s[b] = a_rb[i]
        ark_s[b] = a_rk[i]
        g_row = gc[b[0].start:b[0].start + 1, b[1]]
        g_hi = g_row.astype(BF16).astype(F32)
        split = jnp.where(sel == 0, g_hi, jnp.where(sel == 1, g_row - g_hi, 0.0)).astype(BF16)
        gcol_s[i] = _dot_tn(split, ones_rows)

    def chunk_step(ci, carry_):
        r0 = pl.multiple_of(ci * CHUNK, CHUNK)
        rows = pl.ds(r0, CHUNK)
        groups = [(rows, slice(grp * GROUP, (grp + 1) * GROUP)) for grp in range(n_groups)]
        take = lambda ref: [ref[g] for g in groups]
        y, st = _chunk_state_step(
            take(w_s), take(rt_s), take(u0_s), take(arb_s), take(ark_s), take(vb_s), take(bp_s),
            take(kp_s), [gcol_s[ci * n_groups + grp] for grp in range(n_groups)],
            [state[grp] for grp in range(n_groups)], head_mask, same_head)
        for grp, g in enumerate(groups):
            y_s[g] = y[grp]
            state[grp] = st[grp]
        return carry_

    lax.fori_loop(0, tt // CHUNK, chunk_step, 0)

    y = y_s[...]
    mean = head_sum(y) * (1.0 / HEAD)
    d = y - mean
    var = head_sum(d * d) * (1.0 / HEAD)
    yn = d * lax.rsqrt(var + GN_EPS) * lng_ref[...] + lnb_ref[...]
    y_ref[...] = ((yn + bonus_s[...]) * g_s[...]).astype(y_ref.dtype)


def _rwkv(pr3, vfirst3, prm, tt):
    b, t, _ = pr3.shape
    has_vfirst = vfirst3 is not None
    tile = lambda w: pl.BlockSpec((None, tt, w), lambda i, j: (i, j, 0))
    row = _const_spec((1, D_RWKV))
    lora = _const_spec((LORA_PAD, D_RWKV))
    in_specs = [tile(RWKV_COLS_PAD)]
    args = [pr3]
    if has_vfirst:
        in_specs.append(tile(D_RWKV))
        args.append(vfirst3)
    in_specs += [_const_spec((1, RWKV_COLS_PAD)), row, lora, row, lora, lora]
    args += [prm["mu"], prm["w0"], prm["w_up"], prm["a0"], prm["a_up"], prm["g_up"]]
    if has_vfirst:
        in_specs += [row, lora]
        args += [prm["v0"], prm["v_up"]]
    in_specs += [row, row, row, row, row, _const_spec((tt, tt)), _const_spec((GROUP, GROUP))]
    args += [prm["k_k"], prm["k_a"], prm["r_k"], prm["lnx_g"], prm["lnx_b"],
             prm["ltile"], prm["head_mask"]]
    out_specs = [tile(D_RWKV)]
    out_shape = [jax.ShapeDtypeStruct((b, t, D_RWKV), BF16)]
    if not has_vfirst:
        out_specs.append(tile(D_RWKV))
        out_shape.append(jax.ShapeDtypeStruct((b, t, D_RWKV), F32))
    n_groups = D_RWKV // GROUP
    work_bf16 = pltpu.VMEM((tt, D_RWKV), BF16)
    work_f32 = pltpu.VMEM((tt, D_RWKV), F32)
    outs = pl.pallas_call(
        functools.partial(_rwkv_kernel, tt, has_vfirst),
        grid=(b, t // tt),
        in_specs=in_specs,
        out_specs=out_specs,
        out_shape=out_shape,
        scratch_shapes=[pltpu.VMEM((8, RWKV_COLS_PAD), F32),
                        pltpu.VMEM((n_groups, GROUP, GROUP), F32),
                        pltpu.VMEM((tt // CHUNK * n_groups, GROUP, GROUP), F32)]
                       + [work_bf16] * 10 + [work_f32] * 4,
        compiler_params=_params(("arbitrary", "arbitrary")),
        name="rwkv7_mix",
    )(*args)
    if has_vfirst:
        return outs[0], vfirst3
    return outs[0], outs[1]


def _mem_kv_kernel(mem_ref, g_ref, wkv_ref, kv_ref):
    mn = _rms(mem_ref[...], g_ref[...]).astype(BF16)
    kv_ref[...] = _dot(mn, wkv_ref[...]).astype(kv_ref.dtype)


def _mem_kv(mem2, g, wkv):
    layers = wkv.shape[0]
    m = mem2.shape[0]
    return pl.pallas_call(
        _mem_kv_kernel,
        grid=(layers,),
        in_specs=[_const_spec(mem2.shape), _const_spec((1, D_MODEL)),
                  pl.BlockSpec((None, D_MODEL, 2 * D_MODEL), lambda l: (l, 0, 0))],
        out_specs=pl.BlockSpec((None, m, 2 * D_MODEL), lambda l: (l, 0, 0)),
        out_shape=jax.ShapeDtypeStruct((layers, m, 2 * D_MODEL), BF16),
        compiler_params=_params(("arbitrary",)),
        name="memory_kv",
    )(mem2, g, wkv)


def _xattn_kernel(x_ref, yc_ref, yr_ref, kv_ref, wout_ref, gmix_ref, gpre_ref, wq_ref, wo_ref,
                  gpost_ref, o_ref):
    ymix = _dot(yc_ref[...], wout_ref[0:D_CONV, :]) + _dot(yr_ref[...], wout_ref[D_CONV:, :])
    x = x_ref[...] + _rms(ymix, gmix_ref[...])
    h = _rms(x, gpre_ref[...]).astype(BF16)
    q = (_dot(h, wq_ref[...]) * (XA_HEAD_DIM ** -0.5)).astype(BF16)
    outs = []
    for hd in range(XA_HEADS):
        lo = hd * XA_HEAD_DIM
        s = _dot_nt(q[:, lo:lo + XA_HEAD_DIM], kv_ref[:, lo:lo + XA_HEAD_DIM])
        e = jnp.exp(s - jnp.max(s, axis=-1, keepdims=True))
        pr = (e / jnp.sum(e, axis=-1, keepdims=True)).astype(BF16)
        outs.append(_dot(pr, kv_ref[:, D_MODEL + lo:D_MODEL + lo + XA_HEAD_DIM]))
    o = jnp.concatenate(outs, axis=-1).astype(BF16)
    o_ref[...] = x + _rms(_dot(o, wo_ref[...]), gpost_ref[...])


def _xattn(x3, yc3, yr3, kv3, w_out, gmix, gpre, wq, wo, gpost, tq):
    b, t, _ = x3.shape
    m = kv3.shape[1]
    xs = pl.BlockSpec((None, tq, D_MODEL), lambda i, j: (i, j, 0))
    half = pl.BlockSpec((None, tq, D_CONV), lambda i, j: (i, j, 0))
    gain = _const_spec((1, D_MODEL))
    return pl.pallas_call(
        _xattn_kernel,
        grid=(b, t // tq),
        in_specs=[xs, half, half, pl.BlockSpec((None, m, 2 * D_MODEL), lambda i, j: (i, 0, 0)),
                  _const_spec(w_out.shape), gain, gain, _const_spec(wq.shape), _const_spec(wo.shape),
                  gain],
        out_specs=xs,
        out_shape=jax.ShapeDtypeStruct(x3.shape, F32),
        input_output_aliases={0: 0},
        compiler_params=_params(("arbitrary", "arbitrary")),
        name="out_proj_xattn",
    )(x3, yc3, yr3, kv3, w_out, gmix, gpre, wq, wo, gpost)


def _swiglu_kernel(d_ff, ff_chunk, x_ref, gpre_ref, wgu_ref, wd_ref, gpost_ref, o_ref):
    x = x_ref[...]
    h = _rms(x, gpre_ref[...]).astype(BF16)
    acc = jnp.zeros(x.shape, F32)
    for c0 in range(0, d_ff, ff_chunk):
        gate = _dot(h, wgu_ref[:, c0:c0 + ff_chunk])
        up = _dot(h, wgu_ref[:, d_ff + c0:d_ff + c0 + ff_chunk])
        act = (gate * jax.nn.sigmoid(gate) * up).astype(BF16)
        acc = acc + _dot(act, wd_ref[c0:c0 + ff_chunk, :])
    o_ref[...] = x + _rms(acc, gpost_ref[...])


def _swiglu(x2, gpre, wgu, wd, gpost, tm):
    n = x2.shape[0]
    d_ff = wd.shape[0]
    ff_chunk = next(c for c in (1408, 1024, 512, 256, 128) if d_ff % c == 0)
    xs = pl.BlockSpec((tm, D_MODEL), lambda i: (i, 0))
    single = dict(pipeline_mode=pl.Buffered(1))
    return pl.pallas_call(
        functools.partial(_swiglu_kernel, d_ff, ff_chunk),
        grid=(n // tm,),
        in_specs=[xs, _const_spec((1, D_MODEL)),
                  pl.BlockSpec(wgu.shape, lambda i: (0, 0), **single),
                  pl.BlockSpec(wd.shape, lambda i: (0, 0), **single),
                  _const_spec((1, D_MODEL))],
        out_specs=xs,
        out_shape=jax.ShapeDtypeStruct((n, D_MODEL), F32),
        input_output_aliases={0: 0},
        compiler_params=_params(("arbitrary",)),
        name="swiglu",
    )(x2, gpre, wgu, wd, gpost)


def _pad_rows(w, offset):
    return jnp.zeros((LORA_PAD, w.shape[1]), w.dtype).at[offset:offset + w.shape[0]].set(w)


def _split_w_in(w_in):
    wc = w_in[:, :2 * D_CONV]
    wr = w_in[:, 2 * D_CONV:]
    wr = jnp.pad(wr, ((0, 0), (0, RWKV_COLS_PAD - wr.shape[1])))
    return wc.astype(BF16), wr.astype(BF16)


def kernel(x, mem, mem_norm_g, norm_gains, w_in_first, w_in_rest, mu_first, mu_rest, conv_w, conv_b, conv_ln_g, conv_ln_b, w0, w_up, a0, a_up, g_up, v0, v_up, k_k, k_a, r_k, lnx_g, lnx_b, w_out, wq, wkv, wo, w_gu, w_down):
    b, t, d = x.shape
    depth = norm_gains.shape[0]
    n = b * t
    assert d == D_MODEL and t % CHUNK == 0
    tm = min(512, n)
    tt = min(256, t)
    tq = min(512, t)
    assert n % tm == 0 and t % tt == 0 and t % tq == 0

    row = lambda a: a.reshape(1, -1).astype(F32)
    tpos = jnp.arange(tt)
    same_chunk = (tpos[:, None] // CHUNK) == (tpos[None, :] // CHUNK)
    ltile = (same_chunk & (tpos[:, None] >= tpos[None, :])).astype(BF16)
    gid = jnp.arange(GROUP) // HEAD
    head_mask = (gid[:, None] == gid[None, :]).astype(BF16)

    kv_all = _mem_kv(mem.reshape(b * mem.shape[1], d), row(mem_norm_g), wkv.astype(BF16))
    kv_all = kv_all.reshape(depth, b, mem.shape[1], 2 * d)

    x2 = x.reshape(n, d)
    vfirst = None
    for i in range(depth):
        g = norm_gains[i]
        if i == 0:
            wc, wr = _split_w_in(w_in_first)
            mu = mu_first
        else:
            wc, wr = _split_w_in(w_in_rest[i - 1])
            mu = mu_rest[i - 1]
        mu = jnp.pad(mu, (0, RWKV_COLS_PAD - mu.shape[0]))
        prm = {
            "mu": row(mu), "w0": row(w0[i]), "a0": row(a0[i]),
            "w_up": _pad_rows(w_up[i], 0).astype(BF16),
            "a_up": _pad_rows(a_up[i], w_up.shape[1]).astype(BF16),
            "g_up": g_up[i].astype(BF16),
            "k_k": row(k_k[i]), "k_a": row(k_a[i]), "r_k": row(r_k[i]),
            "lnx_g": row(lnx_g[i]), "lnx_b": row(lnx_b[i]),
            "ltile": ltile, "head_mask": head_mask,
        }
        if i > 0:
            prm["v0"] = row(v0[i - 1])
            prm["v_up"] = _pad_rows(v_up[i - 1], 0).astype(BF16)

        yc, pr = _in_proj_conv(x2.reshape(b, t, d), row(g[0]), wc, wr, conv_w[i], row(conv_b[i]),
                               row(conv_ln_g[i]), row(conv_ln_b[i]), tt)
        yr, vfirst = _rwkv(pr, vfirst, prm, tt)
        x2 = _xattn(x2.reshape(b, t, d), yc, yr, kv_all[i], w_out[i].astype(BF16), row(g[1]),
                    row(g[2]), wq[i].astype(BF16), wo[i].astype(BF16), row(g[3]), tq).reshape(n, d)
        x2 = _swiglu(x2, row(g[4]), w_gu[i].astype(BF16), w_down[i].astype(BF16), row(g[5]), tm)
    return x2.reshape(b, t, d)
```
